```python
import math
import jax, jax.numpy as jnp
from jax import lax
import numpy as np

D_MODEL = 1024
BATCH = 2
SEQ = 8192
DEPTH = 1

GRID_W = 64
HEAD_DIM_A = 128
N_Q_HEADS_A = 8
N_KV_HEADS_A = 2
ROPE_THETA = 10000.0
Q_BLOCK = 128
HEAD_DIM_B = 64
N_HEADS_PER_DIL = 4
DIL_PAIRS = ((128, 1), (512, 4), (2048, 16))
N_HEADS_B = N_HEADS_PER_DIL * len(DIL_PAIRS)
BAND_BLOCK = 64
N_REL_BUCKETS = 32
REL_MAX_DIST = 1024
D_FF = 4 * D_MODEL
D_PLE = 256
NORM_EPS = 1e-6
NEG_INF = -1e30

SPLIT_SIZES = (
    N_Q_HEADS_A * HEAD_DIM_A,
    N_KV_HEADS_A * HEAD_DIM_A,
    N_KV_HEADS_A * HEAD_DIM_A,
    N_HEADS_B * HEAD_DIM_B,
    N_HEADS_B * HEAD_DIM_B,
    N_HEADS_B * HEAD_DIM_B,
    D_MODEL,
    D_MODEL,
)
D_IN_PROJ = sum(SPLIT_SIZES)

kernel_name = "hybrid_gqa_axialrope_dilated_swa_sqrelu_ple"


def rmsnorm(x, g):
    xf = x.astype(jnp.float32)
    y = xf * lax.rsqrt(jnp.mean(xf * xf, axis=-1, keepdims=True) + NORM_EPS)
    return (y * g.astype(jnp.float32)).astype(x.dtype)


def rope_1d(x, pos):
    d = x.shape[-1]
    inv_freq = jnp.power(ROPE_THETA, -jnp.arange(0, d, 2, dtype=jnp.float32) / d)
    ang = pos.astype(jnp.float32)[:, None] * inv_freq[None, :]
    cos = jnp.cos(ang)[None, :, None, :]
    sin = jnp.sin(ang)[None, :, None, :]
    xf = x.astype(jnp.float32)
    x1, x2 = xf[..., : d // 2], xf[..., d // 2:]
    out = jnp.concatenate([x1 * cos - x2 * sin, x2 * cos + x1 * sin], axis=-1)
    return out.astype(x.dtype)


def axial_rope(x, row_ids, col_ids):
    half = x.shape[-1] // 2
    return jnp.concatenate([rope_1d(x[..., :half], row_ids),
                            rope_1d(x[..., half:], col_ids)], axis=-1)


def t5_bucket(rel):
    nb = N_REL_BUCKETS // 2
    ret = (rel > 0).astype(np.int32) * nb
    n = np.abs(rel)
    max_exact = nb // 2
    large = max_exact + (np.log(np.maximum(n, 1) / max_exact)
                         / math.log(REL_MAX_DIST / max_exact)
                         * (nb - max_exact)).astype(np.int32)
    large = np.minimum(large, nb - 1)
    return ret + np.where(n < max_exact, n, large).astype(np.int32)


def mixer_a(q, k, v, q_g, k_g, row_ids, col_ids):
    b, s, hq, hd = q.shape
    hkv = k.shape[2]
    grp = hq // hkv
    q = axial_rope(rmsnorm(q, q_g), row_ids, col_ids) * (hd ** -0.5)
    k = axial_rope(rmsnorm(k, k_g), row_ids, col_ids)
    nblk = s // Q_BLOCK
    qb = q.reshape(b, nblk, Q_BLOCK, hkv, grp, hd).transpose(1, 0, 3, 4, 2, 5)
    kt = k.transpose(0, 2, 1, 3)
    vt = v.transpose(0, 2, 1, 3)

    def attend(qblk):
        sc = jnp.einsum('bkgqd,bksd->bkgqs', qblk, kt).astype(jnp.float32)
        pr = jax.nn.softmax(sc, axis=-1).astype(vt.dtype)
        return jnp.einsum('bkgqs,bksd->bkgqd', pr, vt)

    o = lax.map(attend, qb)
    return o.transpose(1, 0, 4, 2, 3, 5).reshape(b, s, hq * hd)


def dilated_group(q, k, v, bias_table, window, dilation):
    b, s, hh, hd = q.shape
    L = s // dilation
    W = window // (2 * dilation)
    qb_len = math.gcd(L, BAND_BLOCK)
    nb = L // qb_len
    kw_len = qb_len + 2 * W

    def to_sub(t):
        return t.reshape(b, L, dilation, hh, hd).transpose(0, 2, 3, 1, 4)

    off = np.arange(kw_len)[None, :] - W - np.arange(qb_len)[:, None]
    band = np.abs(off) <= W
    kpos = np.arange(nb)[:, None] * qb_len - W + np.arange(kw_len)[None, :]
    inb = (kpos >= 0) & (kpos < L)
    mask = band[None, :, :] & inb[:, None, :]
    bucket = t5_bucket(off * dilation)
    bias = bias_table[bucket].astype(jnp.float32).transpose(2, 0, 1)
    idx = np.arange(nb)[:, None] * qb_len + np.arange(kw_len)[None, :]

    qs = to_sub(q).reshape(b, dilation, hh, nb, qb_len, hd) * (hd ** -0.5)
    pad = ((0, 0), (0, 0), (0, 0), (W, W), (0, 0))
    ks = jnp.pad(to_sub(k), pad)[:, :, :, idx, :]
    vs = jnp.pad(to_sub(v), pad)[:, :, :, idx, :]
    sc = jnp.einsum('bchnqd,bchnkd->bchnqk', qs, ks).astype(jnp.float32) + bias[:, None]
    sc = jnp.where(mask, sc, NEG_INF)
    lse = jax.nn.logsumexp(sc, axis=-1)
    pr = jnp.exp(sc - lse[..., None]).astype(vs.dtype)
    o = jnp.einsum('bchnqk,bchnkd->bchnqd', pr, vs)
    o = o.reshape(b, dilation, hh, L, hd).transpose(0, 3, 1, 2, 4).reshape(b, s, hh, hd)
    lse = lse.reshape(b, dilation, hh, L).transpose(0, 3, 1, 2).reshape(b, s, hh)
    return o, lse


def mixer_b(q, k, v, rel_bias):
    b, s, _, hd = q.shape
    outs, lses = [], []
    for g, (window, dilation) in enumerate(DIL_PAIRS):
        sl = slice(g * N_HEADS_PER_DIL, (g + 1) * N_HEADS_PER_DIL)
        o, l = dilated_group(q[:, :, sl], k[:, :, sl], v[:, :, sl], rel_bias[:, sl], window, dilation)
        outs.append(o)
        lses.append(l)
    wts = jax.nn.softmax(jnp.stack(lses, axis=0), axis=0).astype(q.dtype)
    o = jnp.sum(wts[..., None] * jnp.stack(outs, axis=0), axis=0)
    return o.reshape(b, s, N_HEADS_PER_DIL * hd)


def setup_inputs(seed: int = 0) -> dict:
    key = jax.random.key(seed)
    ks = jax.random.split(key, 20)
    f32 = jnp.float32

    def w(k, shape, fan_in):
        return jax.random.normal(k, shape, f32) * (fan_in ** -0.5)

    def gain(k, shape):
        return 1.0 + 0.05 * jax.random.normal(k, shape, f32)

    return {
        "x": jax.random.normal(ks[0], (BATCH, SEQ, D_MODEL), f32),
        "p": jax.random.normal(ks[1], (DEPTH, BATCH, SEQ, D_PLE), f32),
        "norm_mix_g": gain(ks[2], (DEPTH, D_MODEL)),
        "w_in": w(ks[3], (DEPTH, D_MODEL, D_IN_PROJ), D_MODEL),
        "b_gate": 0.02 * jax.random.normal(ks[4], (DEPTH, 2 * D_MODEL), f32),
        "q_norm_g": gain(ks[5], (DEPTH, HEAD_DIM_A)),
        "k_norm_g": gain(ks[6], (DEPTH, HEAD_DIM_A)),
        "rel_bias": 0.5 * jax.random.normal(ks[7], (N_REL_BUCKETS, N_HEADS_B), f32),
        "w_out_a": w(ks[8], (DEPTH, N_Q_HEADS_A * HEAD_DIM_A, D_MODEL), N_Q_HEADS_A * HEAD_DIM_A),
        "w_out_b": w(ks[9], (DEPTH, N_HEADS_PER_DIL * HEAD_DIM_B, D_MODEL), N_HEADS_PER_DIL * HEAD_DIM_B),
        "w_out": w(ks[10], (DEPTH, D_MODEL, D_MODEL), D_MODEL),
        "norm_mlp_g": gain(ks[11], (DEPTH, D_MODEL)),
        "w_ff1": w(ks[12], (DEPTH, D_MODEL, D_FF), D_MODEL),
        "w_ff2": w(ks[13], (DEPTH, D_FF, D_MODEL), D_FF),
        "norm_ple_g": gain(ks[14], (DEPTH, D_MODEL)),
        "w_ple_gate": w(ks[15], (DEPTH, D_MODEL, D_MODEL), D_MODEL),
        "w_ple": w(ks[16], (DEPTH, D_PLE, D_MODEL), D_PLE),
        "final_norm_g": gain(ks[17], (D_MODEL,)),
    }


def reference(x, p, norm_mix_g, w_in, b_gate, q_norm_g, k_norm_g, rel_bias,
              w_out_a, w_out_b, w_out, norm_mlp_g, w_ff1, w_ff2,
              norm_ple_g, w_ple_gate, w_ple, final_norm_g):
    b, s, _ = x.shape
    rows = s // GRID_W
    row_ids = jnp.repeat(jnp.arange(rows, dtype=jnp.int32), GRID_W)
    col_ids = jnp.arange(s, dtype=jnp.int32) % GRID_W
    split_at = list(np.cumsum(SPLIT_SIZES)[:-1])

    for i in range(DEPTH):
        h = rmsnorm(x, norm_mix_g[i])
        z = h @ w_in[i]
        qa, ka, va, qb, kb, vb, ga, gb = jnp.split(z, split_at, axis=-1)
        qa = qa.reshape(b, s, N_Q_HEADS_A, HEAD_DIM_A)
        ka = ka.reshape(b, s, N_KV_HEADS_A, HEAD_DIM_A)
        va = va.reshape(b, s, N_KV_HEADS_A, HEAD_DIM_A)
        qb = qb.reshape(b, s, N_HEADS_B, HEAD_DIM_B)
        kb = kb.reshape(b, s, N_HEADS_B, HEAD_DIM_B)
        vb = vb.reshape(b, s, N_HEADS_B, HEAD_DIM_B)

        y_a = mixer_a(qa, ka, va, q_norm_g[i], k_norm_g[i], row_ids, col_ids) @ w_out_a[i]
        y_b = mixer_b(qb, kb, vb, rel_bias) @ w_out_b[i]
        gate_a = jax.nn.sigmoid(ga + b_gate[i, :D_MODEL])
        gate_b = jax.nn.sigmoid(gb + b_gate[i, D_MODEL:])
        x = x + (gate_a * y_a + gate_b * y_b) @ w_out[i]

        h = rmsnorm(x, norm_mlp_g[i])
        x = x + jnp.square(jax.nn.relu(h @ w_ff1[i])) @ w_ff2[i]

        gate_p = jax.nn.sigmoid(rmsnorm(x, norm_ple_g[i]) @ w_ple_gate[i])
        x = x + gate_p * (p[i] @ w_ple[i])

    return rmsnorm(x, final_norm_g)
```

```python
import functools
import math

import numpy as np
import jax
import jax.numpy as jnp
from jax import lax
from jax.experimental import pallas as pl
from jax.experimental.pallas import tpu as pltpu

GRID_W = 64
HEAD_DIM_A = 128
N_Q_HEADS_A = 8
N_KV_HEADS_A = 2
GQA_GROUP = N_Q_HEADS_A // N_KV_HEADS_A
ROPE_THETA = 10000.0
HEAD_DIM_B = 64
N_HEADS_PER_DIL = 4
DIL_PAIRS = ((128, 1), (512, 4), (2048, 16))
N_HEADS_B = N_HEADS_PER_DIL * len(DIL_PAIRS)
N_REL_BUCKETS = 32
REL_MAX_DIST = 1024
NORM_EPS = 1e-6
NEG_INF = -1e30
LOG2E = math.log2(math.e)

QA_W = N_Q_HEADS_A * HEAD_DIM_A
KVA_W = N_KV_HEADS_A * HEAD_DIM_A
QB_W = N_HEADS_B * HEAD_DIM_B
GRP_W = N_HEADS_PER_DIL * HEAD_DIM_B

V7X_LANES = 128
V7X_VMEM_LIMIT_BYTES = 56 * 1024 * 1024

INPROJ_ROWS = 512
ATTN_Q_TILE = 512
ATTN_KV_CHUNK = 512
DIL_Q_BLOCK = 128
DIL_HALF = 64
DIL_WIN = DIL_Q_BLOCK + 2 * DIL_HALF
DIL_ROWS = 1024
POST_ROWS = 256

F32 = jnp.float32
BF16 = jnp.bfloat16


def _const_spec(shape):
    zeros = (0,) * len(shape)
    return pl.BlockSpec(shape, lambda *_: zeros, pipeline_mode=pl.Buffered(1))


def _params(n_axes):
    return pltpu.CompilerParams(dimension_semantics=("arbitrary",) * n_axes,
                                vmem_limit_bytes=V7X_VMEM_LIMIT_BYTES)


def _rms(x, g):
    return x * lax.rsqrt(jnp.mean(x * x, axis=-1, keepdims=True) + NORM_EPS) * g


def _inproj_kernel(x_ref, g_ref, wqv_ref, wn_ref, bg_ref, qg_ref, kg_ref,
                   cq_ref, sq_ref, ck_ref, sk_ref,
                   qt_ref, ka_ref, vt_ref, qb_ref, kb_ref, vb_ref, ga_ref, gb_ref):
    h = _rms(x_ref[0], g_ref[...]).astype(BF16)

    zt = lax.dot_general(wqv_ref[...], h, (((1,), (1,)), ((), ())),
                         preferred_element_type=F32)
    cq, sq, qg = cq_ref[...], sq_ref[...], qg_ref[...]
    hd, q4 = HEAD_DIM_A, HEAD_DIM_A // 4
    for hh in range(N_Q_HEADS_A):
        blk = zt[hh * hd:(hh + 1) * hd, :]
        y = blk * lax.rsqrt(jnp.mean(blk * blk, axis=0, keepdims=True) + NORM_EPS) * qg
        partner = jnp.concatenate([y[q4:2 * q4], y[0:q4], y[3 * q4:4 * q4], y[2 * q4:3 * q4]], axis=0)
        qt_ref[0, hh * hd:(hh + 1) * hd, :] = (y * cq + partner * sq).astype(BF16)
    vt_ref[0] = zt[QA_W:QA_W + KVA_W, :].astype(BF16)

    zk = jnp.dot(h, wn_ref[:, 0:KVA_W], preferred_element_type=F32)
    ck, sk, kg = ck_ref[...], sk_ref[...], kg_ref[...]
    lane = lax.broadcasted_iota(jnp.int32, (zk.shape[0], hd), 1)
    first_quarter = (lane % (2 * q4)) < q4
    for kh in range(N_KV_HEADS_A):
        blk = zk[:, kh * hd:(kh + 1) * hd]
        y = blk * lax.rsqrt(jnp.mean(blk * blk, axis=-1, keepdims=True) + NORM_EPS) * kg
        partner = jnp.where(first_quarter, pltpu.roll(y, hd - q4, 1), pltpu.roll(y, q4, 1))
        ka_ref[0, :, kh * hd:(kh + 1) * hd] = (y * ck + partner * sk).astype(BF16)

    c0 = KVA_W
    zb = jnp.dot(h, wn_ref[:, c0:c0 + 3 * QB_W], preferred_element_type=F32)
    qb_ref[0] = (zb[:, 0:QB_W] * (HEAD_DIM_B ** -0.5)).astype(BF16)
    kb_ref[0] = zb[:, QB_W:2 * QB_W].astype(BF16)
    vb_ref[0] = zb[:, 2 * QB_W:3 * QB_W].astype(BF16)

    c1 = c0 + 3 * QB_W
    d = ga_ref.shape[-1]
    zg = jnp.dot(h, wn_ref[:, c1:c1 + 2 * d], preferred_element_type=F32) + bg_ref[...]
    gate = jax.nn.sigmoid(zg)
    ga_ref[0] = gate[:, 0:d].astype(BF16)
    gb_ref[0] = gate[:, d:2 * d].astype(BF16)


def _inproj(x, g, wqv_t, wn, bg, qg_col, kg_row, cq_t, sq_t, ck, sk):
    b, s, d = x.shape
    tm = INPROJ_ROWS
    assert s % tm == 0
    n_nat = wn.shape[1]
    row = lambda i, bb: (bb, i, 0)
    out_shape = (
        jax.ShapeDtypeStruct((b, QA_W, s), BF16),
        jax.ShapeDtypeStruct((b, s, KVA_W), BF16),
        jax.ShapeDtypeStruct((b, KVA_W, s), BF16),
        jax.ShapeDtypeStruct((b, s, QB_W), BF16),
        jax.ShapeDtypeStruct((b, s, QB_W), BF16),
        jax.ShapeDtypeStruct((b, s, QB_W), BF16),
        jax.ShapeDtypeStruct((b, s, d), BF16),
        jax.ShapeDtypeStruct((b, s, d), BF16),
    )
    in_specs = [
        pl.BlockSpec((1, tm, d), row),
        _const_spec((1, d)),
        _const_spec((QA_W + KVA_W, d)),
        _const_spec((d, n_nat)),
        _const_spec((1, 2 * d)),
        _const_spec((HEAD_DIM_A, 1)),
        _const_spec((1, HEAD_DIM_A)),
        pl.BlockSpec((HEAD_DIM_A, tm), lambda i, bb: (0, i)),
        pl.BlockSpec((HEAD_DIM_A, tm), lambda i, bb: (0, i)),
        pl.BlockSpec((tm, HEAD_DIM_A), lambda i, bb: (i, 0)),
        pl.BlockSpec((tm, HEAD_DIM_A), lambda i, bb: (i, 0)),
    ]
    out_specs = (
        pl.BlockSpec((1, QA_W, tm), lambda i, bb: (bb, 0, i)),
        pl.BlockSpec((1, tm, KVA_W), row),
        pl.BlockSpec((1, KVA_W, tm), lambda i, bb: (bb, 0, i)),
        pl.BlockSpec((1, tm, QB_W), row),
        pl.BlockSpec((1, tm, QB_W), row),
        pl.BlockSpec((1, tm, QB_W), row),
        pl.BlockSpec((1, tm, d), row),
        pl.BlockSpec((1, tm, d), row),
    )
    return pl.pallas_call(
        _inproj_kernel, out_shape=out_shape, grid=(s // tm, b),
        in_specs=in_specs, out_specs=out_specs,
        compiler_params=_params(2), name="inproj",
    )(x, g, wqv_t, wn, bg, qg_col, kg_row, cq_t, sq_t, ck, sk)


def _attn_a_kernel(qt_ref, k_ref, vt_ref, o_ref, acc_ref, m_ref, l_ref, *, tk):
    hd = HEAD_DIM_A
    s_len = k_ref.shape[1]
    acc_ref[...] = jnp.zeros(acc_ref.shape, F32)
    m_ref[...] = jnp.full(m_ref.shape, NEG_INF, F32)
    l_ref[...] = jnp.zeros(l_ref.shape, F32)

    def chunk(c, carry):
        k0 = pl.multiple_of(c * tk, tk)
        k_c = k_ref[0, pl.ds(k0, tk), :]
        vt_c = vt_ref[0, :, pl.ds(k0, tk)]
        for hh in range(GQA_GROUP):
            q_h = qt_ref[0, hh * hd:(hh + 1) * hd, :]
            s = jnp.dot(k_c, q_h, preferred_element_type=F32)
            m_old = m_ref[hh:hh + 1, :]
            m_new = jnp.maximum(m_old, jnp.max(s, axis=0, keepdims=True))
            alpha = jnp.exp2(m_old - m_new)
            p = jnp.exp2(s - m_new)
            l_ref[hh:hh + 1, :] = alpha * l_ref[hh:hh + 1, :] + jnp.sum(p, axis=0, keepdims=True)
            m_ref[hh:hh + 1, :] = m_new
            pv = jnp.dot(vt_c, p.astype(BF16), preferred_element_type=F32)
            acc_ref[hh] = alpha * acc_ref[hh] + pv
        return carry

    lax.fori_loop(0, s_len // tk, chunk, 0)

    for hh in range(GQA_GROUP):
        o_t = acc_ref[hh] / l_ref[hh:hh + 1, :]
        o_ref[0, :, hh * hd:(hh + 1) * hd] = o_t.T.astype(BF16)


def _attn_a(qt, ka, vt):
    b, _, s = qt.shape
    tq, tk = ATTN_Q_TILE, ATTN_KV_CHUNK
    assert s % tq == 0 and s % tk == 0
    gw = GQA_GROUP * HEAD_DIM_A
    return pl.pallas_call(
        functools.partial(_attn_a_kernel, tk=tk),
        out_shape=jax.ShapeDtypeStruct((b, s, QA_W), BF16),
        grid=(b, N_KV_HEADS_A, s // tq),
        in_specs=[
            pl.BlockSpec((1, gw, tq), lambda bb, kv, i: (bb, kv, i)),
            pl.BlockSpec((1, s, HEAD_DIM_A), lambda bb, kv, i: (bb, 0, kv)),
            pl.BlockSpec((1, HEAD_DIM_A, s), lambda bb, kv, i: (bb, kv, 0)),
        ],
        out_specs=pl.BlockSpec((1, tq, gw), lambda bb, kv, i: (bb, i, kv)),
        scratch_shapes=[
            pltpu.VMEM((GQA_GROUP, HEAD_DIM_A, tq), F32),
            pltpu.VMEM((GQA_GROUP, tq), F32),
            pltpu.VMEM((GQA_GROUP, tq), F32),
        ],
        compiler_params=_params(3), name="attn_a",
    )(qt, ka, vt)


def _t5_bucket(rel):
    nb = N_REL_BUCKETS // 2
    ret = (rel > 0).astype(np.int32) * nb
    n = np.abs(rel)
    max_exact = nb // 2
    large = max_exact + (np.log(np.maximum(n, 1) / max_exact)
                         / math.log(REL_MAX_DIST / max_exact)
                         * (nb - max_exact)).astype(np.int32)
    large = np.minimum(large, nb - 1)
    return ret + np.where(n < max_exact, n, large).astype(np.int32)


def _dil_bucket_table(dilation):
    q = np.arange(DIL_Q_BLOCK)[:, None]
    k = np.arange(DIL_WIN)[None, :]
    out = []
    for shift in (-DIL_HALF, 0, -2 * DIL_HALF):
        off = k + shift - q
        out.append(np.where(np.abs(off) <= DIL_HALF, _t5_bucket(off * dilation), -1))
    return np.stack(out).astype(np.int32)


def _dil_kernel(*refs, group, seq_l, tm, merge):
    if merge:
        (tab_ref, bucket_ref, q_ref, k_ref, v_ref, o1_ref, l1_ref, o2_ref, l2_ref,
         out_ref, bias_ref) = refs
    else:
        tab_ref, bucket_ref, q_ref, k_ref, v_ref, o_ref, lse_ref, bias_ref = refs
    nq, nk, hb = DIL_Q_BLOCK, DIL_WIN, HEAD_DIM_B
    first = (pl.program_id(0) == 0) & (pl.program_id(1) == 0) & (pl.program_id(2) == 0)

    @pl.when(first)
    def _():
        for var in range(3):
            bk = bucket_ref[var]
            acc = [jnp.full((nq, nk), NEG_INF, F32) for _ in range(N_HEADS_PER_DIL)]
            for bidx in range(N_REL_BUCKETS):
                hit = bk == bidx
                for j in range(N_HEADS_PER_DIL):
                    acc[j] = jnp.where(hit, tab_ref[bidx, group * N_HEADS_PER_DIL + j], acc[j])
            for j in range(N_HEADS_PER_DIL):
                bias_ref[var, j] = acc[j]

    lane = lax.broadcasted_iota(jnp.int32, (nq, GRP_W), 1)
    head_of_lane = lane // hb
    n_blocks = seq_l // nq
    i = pl.program_id(2)

    def block(jb, carry):
        gbi = i * (tm // nq) + jb
        r0 = pl.multiple_of(jb * nq, nq)
        ws = pl.multiple_of(jnp.clip(gbi * nq - DIL_HALF, 0, seq_l - nk), DIL_HALF)
        var = jnp.where(gbi == 0, 1, jnp.where(gbi == n_blocks - 1, 2, 0))
        qblk = q_ref[0, pl.ds(r0, nq), :]
        kw = k_ref[0, pl.ds(ws, nk), :]
        vw = v_ref[0, pl.ds(ws, nk), :]
        o_acc = jnp.zeros((nq, GRP_W), F32)
        lse_acc = jnp.zeros((nq, GRP_W), F32)
        for j in range(N_HEADS_PER_DIL):
            mine = head_of_lane == j
            qm = jnp.where(mine, qblk, jnp.zeros_like(qblk))
            s = lax.dot_general(qm, kw, (((1,), (1,)), ((), ())),
                                preferred_element_type=F32) + bias_ref[var, j]
            m = jnp.max(s, axis=-1, keepdims=True)
            p = jnp.exp(s - m)
            l = jnp.sum(p, axis=-1, keepdims=True)
            oj = jnp.dot(p.astype(BF16), vw, preferred_element_type=F32)
            o_acc = jnp.where(mine, oj / l, o_acc)
            lse_acc = jnp.where(mine, m + jnp.log(l), lse_acc)
        if merge:
            rows = pl.ds(r0, nq)
            l1, l2 = l1_ref[0, rows, :], l2_ref[0, rows, :]
            top = jnp.maximum(lse_acc, jnp.maximum(l1, l2))
            e0, e1, e2 = jnp.exp(lse_acc - top), jnp.exp(l1 - top), jnp.exp(l2 - top)
            mixed = (e0 * o_acc + e1 * o1_ref[0, rows, :] + e2 * o2_ref[0, rows, :]) / (e0 + e1 + e2)
            out_ref[0, rows, :] = mixed.astype(BF16)
        else:
            o_ref[0, pl.ds(r0, nq), :] = o_acc
            lse_ref[0, pl.ds(r0, nq), :] = lse_acc
        return carry

    lax.fori_loop(0, tm // nq, block, 0)


def _dil_group(rel_bias, qb, kb, vb, group, merge_with=None):
    window, dil = DIL_PAIRS[group]
    assert window == 2 * DIL_HALF * dil
    b, s, _ = qb.shape
    seq_l = s // dil
    tm = min(DIL_ROWS, seq_l)
    assert seq_l % tm == 0 and tm % DIL_Q_BLOCK == 0 and seq_l >= 2 * DIL_WIN
    merge = merge_with is not None
    assert not merge or dil == 1
    n_col = QB_W // GRP_W
    view = lambda t: t.reshape(b, seq_l, dil * t.shape[-1])
    bucket = jnp.asarray(_dil_bucket_table(dil))

    qspec = pl.BlockSpec((1, tm, GRP_W), lambda bb, r, i: (bb, i, r * n_col + group))
    kvspec = pl.BlockSpec((1, seq_l, GRP_W), lambda bb, r, i: (bb, 0, r * n_col + group))
    ospec = pl.BlockSpec((1, tm, GRP_W), lambda bb, r, i: (bb, i, r))
    in_specs = [
        pl.BlockSpec(memory_space=pltpu.SMEM),
        _const_spec(bucket.shape),
        qspec, kvspec, kvspec,
    ]
    args = [rel_bias, bucket, view(qb), view(kb), view(vb)]
    if merge:
        for o_g, l_g in merge_with:
            in_specs += [ospec, ospec]
            args += [o_g, l_g]
        out_shape = jax.ShapeDtypeStruct((b, seq_l, dil * GRP_W), BF16)
        out_specs = ospec
    else:
        out_shape = (jax.ShapeDtypeStruct((b, seq_l, dil * GRP_W), F32),) * 2
        out_specs = (ospec, ospec)
    res = pl.pallas_call(
        functools.partial(_dil_kernel, group=group, seq_l=seq_l, tm=tm, merge=merge),
        out_shape=out_shape, grid=(b, dil, seq_l // tm),
        in_specs=in_specs, out_specs=out_specs,
        scratch_shapes=[pltpu.VMEM((3, N_HEADS_PER_DIL, DIL_Q_BLOCK, DIL_WIN), F32)],
        compiler_params=_params(3), name=f"dil_d{dil}",
    )(*args)
    if merge:
        return res.reshape(b, s, GRP_W)
    return tuple(t.reshape(b, s, GRP_W) for t in res)


def _post_kernel(a_ref, bm_ref, ga_ref, gb_ref, x_ref, p_ref,
                 wa_ref, wb_ref, wo_ref, gm_ref, w1_ref, w2_ref, gp_ref, wpg_ref, wp_ref, gf_ref,
                 out_ref, *, last):
    dot = functools.partial(jnp.dot, preferred_element_type=F32)
    ya = dot(a_ref[0], wa_ref[...])
    yb = dot(bm_ref[0], wb_ref[...])
    mix = ga_ref[0].astype(F32) * ya + gb_ref[0].astype(F32) * yb
    x1 = x_ref[0] + dot(mix.astype(BF16), wo_ref[...])

    h2 = _rms(x1, gm_ref[...]).astype(BF16)
    u = jnp.square(jnp.maximum(dot(h2, w1_ref[...]), 0.0)).astype(BF16)
    x2 = x1 + dot(u, w2_ref[...])

    h3 = _rms(x2, gp_ref[...]).astype(BF16)
    gate_p = jax.nn.sigmoid(dot(h3, wpg_ref[...]))
    x3 = x2 + gate_p * dot(p_ref[0].astype(BF16), wp_ref[...])
    out_ref[0] = _rms(x3, gf_ref[...]) if last else x3


def _post(a, bm, ga, gb, x, p, wa, wb, wo, gm, w1, w2, gp, wpg, wp, gf, last):
    b, s, d = x.shape
    tm = POST_ROWS
    assert s % tm == 0
    row = lambda bb, i: (bb, i, 0)
    tile = lambda t: pl.BlockSpec((1, tm, t.shape[-1]), row)
    weights = (wa, wb, wo, gm, w1, w2, gp, wpg, wp, gf)
    return pl.pallas_call(
        functools.partial(_post_kernel, last=last),
        out_shape=jax.ShapeDtypeStruct((b, s, d), F32),
        grid=(b, s // tm),
        in_specs=[tile(t) for t in (a, bm, ga, gb, x, p)] + [_const_spec(w.shape) for w in weights],
        out_specs=pl.BlockSpec((1, tm, d), row),
        compiler_params=_params(2), name="post",
    )(a, bm, ga, gb, x, p, *weights)


def _rope_tables(s):
    half = HEAD_DIM_A // 2
    inv_freq = jnp.power(ROPE_THETA, -jnp.arange(0, half, 2, dtype=F32) / half)
    rows = jnp.repeat(jnp.arange(s // GRID_W, dtype=jnp.int32), GRID_W)
    cols = jnp.arange(s, dtype=jnp.int32) % GRID_W
    ang_r = rows.astype(F32)[:, None] * inv_freq[None, :]
    ang_c = cols.astype(F32)[:, None] * inv_freq[None, :]
    cos = jnp.concatenate([jnp.cos(ang_r), jnp.cos(ang_r), jnp.cos(ang_c), jnp.cos(ang_c)], axis=-1)
    sin = jnp.concatenate([-jnp.sin(ang_r), jnp.sin(ang_r), -jnp.sin(ang_c), jnp.sin(ang_c)], axis=-1)
    return cos, sin


def kernel(x, p, norm_mix_g, w_in, b_gate, q_norm_g, k_norm_g, rel_bias, w_out_a, w_out_b, w_out,
           norm_mlp_g, w_ff1, w_ff2, norm_ple_g, w_ple_gate, w_ple, final_norm_g):
    b, s, d = x.shape
    depth = w_in.shape[0]
    cos, sin = _rope_tables(s)
    q_scale = (HEAD_DIM_A ** -0.5) * LOG2E
    cq_t, sq_t = (cos * q_scale).T, (sin * q_scale).T
    split = np.cumsum((QA_W, KVA_W, KVA_W, QB_W, QB_W, QB_W, d, d))

    for i in range(depth):
        w = w_in[i]
        wqv_t = jnp.concatenate([w[:, :split[0]], w[:, split[1]:split[2]]], axis=1).T.astype(BF16)
        wn = jnp.concatenate([w[:, split[0]:split[1]], w[:, split[2]:]], axis=1).astype(BF16)
        qt, ka, vt, qb, kb, vb, ga, gb = _inproj(
            x, norm_mix_g[i][None, :], wqv_t, wn, b_gate[i][None, :],
            q_norm_g[i][:, None], k_norm_g[i][None, :], cq_t, sq_t, cos, sin)

        a_out = _attn_a(qt, ka, vt)

        far = [_dil_group(rel_bias, qb, kb, vb, g) for g in (1, 2)]
        b_out = _dil_group(rel_bias, qb, kb, vb, 0, merge_with=far)

        x = _post(a_out, b_out, ga, gb, x, p[i],
                  w_out_a[i].astype(BF16), w_out_b[i].astype(BF16), w_out[i].astype(BF16),
                  norm_mlp_g[i][None, :], w_ff1[i].astype(BF16), w_ff2[i].astype(BF16),
                  norm_ple_g[i][None, :], w_ple_gate[i].astype(BF16), w_ple[i].astype(BF16),
                  final_norm_g[None, :], last=(i == depth - 1))
    return x
```
